```python
import jax, jax.numpy as jnp
from jax import lax
import numpy as np

D_MODEL = 1024
BATCH = 2
SEQ = 8192
DEPTH = 1
DEC_BATCH = 16
DEC_SEQ = 16
PAST_LEN = 1024

CHUNK = 64
QBLOCK = 128
N_HEADS_A = 8
HEAD_DIM_A = 64
IDX_HEADS = 4
IDX_DIM = 64
TOPK_MAX = 256
N_HEADS_R = 4
DK_R = 64
DV_R = 128
N_GROUPS = 4
EXPERTS_PER_GROUP = 4
N_EXPERTS = N_GROUPS * EXPERTS_PER_GROUP
TOP_E = 2
D_FF_E = 512
EPS = 1e-6

WA = N_HEADS_A * HEAD_DIM_A
WQI = IDX_HEADS * IDX_DIM
WQR = N_HEADS_R * DK_R
WVR = N_HEADS_R * DV_R
SPLIT_SIZES = (WA, WA, WA, WQI, IDX_DIM, IDX_HEADS, WQR, WQR, WVR, WVR, D_MODEL, D_MODEL)
IN_COLS = WA * 3 + WQI + IDX_DIM + IDX_HEADS + WQR * 2 + WVR * 2 + D_MODEL * 2

kernel_name = 'hybrid_dsa_retention_hmoe_stream_step'


def _split_points():
    pts, acc = [], 0
    for s in SPLIT_SIZES[:-1]:
        acc += s
        pts.append(acc)
    return pts


def rmsnorm(x, g):
    xf = x.astype(jnp.float32)
    y = xf * lax.rsqrt(jnp.mean(xf * xf, axis=-1, keepdims=True) + EPS)
    return (y * g.astype(jnp.float32)).astype(x.dtype)


def alibi_slopes():
    h = jnp.arange(N_HEADS_A, dtype=jnp.float32)
    return 2.0 ** (-8.0 * (h + 1.0) / N_HEADS_A)


def dsa_block(q, qi, wi, qpos, k, v, ki, kpos, topk):
    admissible = (kpos[None, :] // CHUNK) <= (qpos[:, None] // CHUNK)
    idx_logits = jnp.einsum('bqhd,bsd->bqhs', qi, ki).astype(jnp.float32) * (IDX_DIM ** -0.5)
    score = jnp.einsum('bqhs,bqh->bqs', jax.nn.relu(idx_logits), wi.astype(jnp.float32))
    score = jnp.where(admissible[None], score, -jnp.inf)
    _, sel = lax.top_k(score, topk)
    sel_pos = kpos[sel]
    valid = (sel_pos // CHUNK) <= (qpos[None, :, None] // CHUNK)
    kg = jax.vmap(lambda kb, ib: kb[ib])(k, sel)
    vg = jax.vmap(lambda vb, ib: vb[ib])(v, sel)
    logits = jnp.einsum('bqhd,bqkhd->bhqk', q, kg).astype(jnp.float32) * (HEAD_DIM_A ** -0.5)
    dist = jnp.abs(qpos[None, :, None] - sel_pos).astype(jnp.float32)
    logits = logits - alibi_slopes()[None, :, None, None] * dist[:, None]
    logits = jnp.where(valid[:, None], logits, -1e30)
    p = jax.nn.softmax(logits, axis=-1).astype(v.dtype)
    return jnp.einsum('bhqk,bqkhd->bqhd', p, vg)


def dsa_attention(q, qi, wi, k, v, ki, pos0):
    B, T = q.shape[0], q.shape[1]
    L = k.shape[1]
    topk = min(TOPK_MAX, L // 4)
    kpos = jnp.arange(L, dtype=jnp.int32)
    qpos = pos0 + jnp.arange(T, dtype=jnp.int32)
    qb = min(QBLOCK, T)
    nb = T // qb

    def split(a):
        return jnp.moveaxis(a.reshape((B, nb, qb) + a.shape[2:]), 1, 0)

    def blk(args):
        qq, qqi, wwi, pp = args
        return dsa_block(qq, qqi, wwi, pp, k, v, ki, kpos, topk)

    out = lax.map(blk, (split(q), split(qi), split(wi), qpos.reshape(nb, qb)))
    return jnp.moveaxis(out, 0, 1).reshape((B, T) + q.shape[2:])


def retention_chunk(S, q, k, v, log_gamma):
    C = q.shape[2]
    i = jnp.arange(C, dtype=jnp.float32)
    rel = i[:, None] - i[None, :]
    decay = jnp.where(rel >= 0, jnp.exp(jnp.maximum(rel, 0.0)[None] * log_gamma[:, None, None]), 0.0)
    inner = jnp.einsum('bhid,bhjd->bhij', q, k) * decay.astype(q.dtype)[None]
    o = jnp.einsum('bhij,bhjv->bhiv', inner, v)
    cross = jnp.exp((i + 1.0)[None, :] * log_gamma[:, None]).astype(q.dtype)
    o = o + jnp.einsum('bhid,bhdv->bhiv', q, S) * cross[None, :, :, None]
    kdec = jnp.exp((C - 1.0 - i)[None, :] * log_gamma[:, None]).astype(k.dtype)
    sdec = jnp.exp(C * log_gamma).astype(S.dtype)
    S_new = S * sdec[None, :, None, None] + jnp.einsum('bhjd,bhjv->bhdv', k * kdec[None, :, :, None], v)
    return S_new.astype(S.dtype), o


def retention(q, k, v, S0):
    B, T = q.shape[0], q.shape[1]
    c = min(CHUNK, T)
    nc = T // c
    log_gamma = jnp.log1p(-(2.0 ** (-5.0 - jnp.arange(N_HEADS_R, dtype=jnp.float32))))

    def to_chunks(a):
        return a.reshape(B, nc, c, N_HEADS_R, a.shape[-1]).transpose(1, 0, 3, 2, 4)

    def step(S, xs):
        qc, kc, vc = xs
        return retention_chunk(S, qc, kc, vc, log_gamma)

    S_fin, o = lax.scan(step, S0, (to_chunks(q), to_chunks(k), to_chunks(v)))
    o = o.transpose(1, 0, 3, 2, 4).reshape(B, T, N_HEADS_R, DV_R)
    return o, S_fin


def token_mixer(h, past_k, past_v, past_ki, s0, w_in, qn_g, kn_g, kidx_g, ret_norm_g, w_pa, w_pb, w_out):
    B, T, _ = h.shape
    u = h @ w_in
    qa, ka, va, qi, ki, wi, qr, kr, vr, gr, ga, gb = jnp.split(u, _split_points(), axis=-1)
    qa = rmsnorm(qa.reshape(B, T, N_HEADS_A, HEAD_DIM_A), qn_g)
    ka = rmsnorm(ka.reshape(B, T, N_HEADS_A, HEAD_DIM_A), kn_g)
    va = va.reshape(B, T, N_HEADS_A, HEAD_DIM_A)
    qi = qi.reshape(B, T, IDX_HEADS, IDX_DIM)
    ki = rmsnorm(ki, kidx_g)
    wi = wi * (IDX_HEADS ** -0.5)
    pos0 = past_k.shape[1]
    k_all = jnp.concatenate([past_k, ka], axis=1)
    v_all = jnp.concatenate([past_v, va], axis=1)
    ki_all = jnp.concatenate([past_ki, ki], axis=1)
    oa = dsa_attention(qa, qi, wi, k_all, v_all, ki_all, pos0).reshape(B, T, WA)
    qr = qr.reshape(B, T, N_HEADS_R, DK_R)
    kr = kr.reshape(B, T, N_HEADS_R, DK_R) * (DK_R ** -0.5)
    vr = vr.reshape(B, T, N_HEADS_R, DV_R)
    orr, s_new = retention(qr, kr, vr, s0)
    orr = rmsnorm(orr, ret_norm_g).reshape(B, T, WVR) * jax.nn.silu(gr)
    merged = jax.nn.sigmoid(ga) * (oa @ w_pa) + jax.nn.sigmoid(gb) * (orr @ w_pb)
    return merged @ w_out, (ka, va, ki, s_new)


def hier_moe(h, w_rg, b_rg, w_re, b_re, w_e_gate, w_e_up, w_e_down):
    B, T, D = h.shape
    xt = h.reshape(B * T, D)
    N = B * T
    g_prob = jax.nn.softmax((xt @ w_rg + b_rg).astype(jnp.float32), axis=-1)
    g_top_p, g_top = lax.top_k(g_prob, 1)
    e_logits = (xt @ w_re + b_re).astype(jnp.float32).reshape(N, N_GROUPS, EXPERTS_PER_GROUP)
    e_in = jnp.einsum('nge,ng->ne', e_logits, jax.nn.one_hot(g_top[:, 0], N_GROUPS, dtype=jnp.float32))
    e_prob = jax.nn.softmax(e_in, axis=-1)
    e_top_p, e_top = lax.top_k(e_prob, TOP_E)
    e_top_p = e_top_p / jnp.sum(e_top_p, axis=-1, keepdims=True)
    weight = g_top_p * e_top_p
    expert_id = g_top * EXPERTS_PER_GROUP + e_top
    combine = jnp.sum(jax.nn.one_hot(expert_id, N_EXPERTS, dtype=jnp.float32) * weight[..., None], axis=1)
    combine = combine.astype(xt.dtype)
    y = jnp.zeros_like(xt)
    for e in range(N_EXPERTS):
        he = jax.nn.silu(xt @ w_e_gate[e]) * (xt @ w_e_up[e])
        y = y + combine[:, e:e + 1] * (he @ w_e_down[e])
    return y.reshape(B, T, D)


def layer(x, c, past_k, past_v, past_ki, s0, w_ada, b_ada, norm1_g, norm2_g, w_in, qn_g, kn_g,
          kidx_g, ret_norm_g, w_pa, w_pb, w_out, w_rg, b_rg, w_re, b_re, w_e_gate, w_e_up, w_e_down):
    mod = (jax.nn.silu(c) @ w_ada + b_ada)[:, None, :]
    sh1, sc1, gt1, sh2, sc2, gt2 = jnp.split(mod, 6, axis=-1)
    h = rmsnorm(x, norm1_g) * (1.0 + sc1) + sh1
    y, new_state = token_mixer(h, past_k, past_v, past_ki, s0, w_in, qn_g, kn_g, kidx_g,
                               ret_norm_g, w_pa, w_pb, w_out)
    x = x + gt1 * y
    h2 = rmsnorm(x, norm2_g) * (1.0 + sc2) + sh2
    x = x + gt2 * hier_moe(h2, w_rg, b_rg, w_re, b_re, w_e_gate, w_e_up, w_e_down)
    return x, new_state


def setup_inputs(seed: int = 0) -> dict:
    key = jax.random.key(seed)
    ks = jax.random.split(key, 32)
    f32 = jnp.float32
    D = D_MODEL
    L = DEPTH

    def nrm(k, shape, scale):
        return jax.random.normal(k, shape, f32) * scale

    return {
        'x_prompt': nrm(ks[0], (BATCH, SEQ, D), 1.0),
        'x_sample': nrm(ks[1], (DEC_BATCH, DEC_SEQ, D), 1.0),
        'cache_k': nrm(ks[2], (L, DEC_BATCH, PAST_LEN, N_HEADS_A, HEAD_DIM_A), 1.0),
        'cache_v': nrm(ks[3], (L, DEC_BATCH, PAST_LEN, N_HEADS_A, HEAD_DIM_A), 1.0),
        'cache_kidx': nrm(ks[4], (L, DEC_BATCH, PAST_LEN, IDX_DIM), 1.0),
        'state_ret': nrm(ks[5], (L, DEC_BATCH, N_HEADS_R, DK_R, DV_R), 0.5),
        'c_prompt': nrm(ks[6], (BATCH, D), 1.0),
        'c_sample': nrm(ks[7], (DEC_BATCH, D), 1.0),
        'w_ada': nrm(ks[8], (L, D, 6 * D), 0.2 * D ** -0.5),
        'b_ada': nrm(ks[9], (L, 6 * D), 0.02),
        'norm1_g': 1.0 + nrm(ks[10], (L, D), 0.02),
        'norm2_g': 1.0 + nrm(ks[11], (L, D), 0.02),
        'w_in': nrm(ks[12], (L, D, IN_COLS), D ** -0.5),
        'qn_g': 1.0 + nrm(ks[13], (L, HEAD_DIM_A), 0.02),
        'kn_g': 1.0 + nrm(ks[14], (L, HEAD_DIM_A), 0.02),
        'kidx_g': 1.0 + nrm(ks[15], (L, IDX_DIM), 0.02),
        'ret_norm_g': 1.0 + nrm(ks[16], (L, DV_R), 0.02),
        'w_pa': nrm(ks[17], (L, WA, D), WA ** -0.5),
        'w_pb': nrm(ks[18], (L, WVR, D), WVR ** -0.5),
        'w_out': nrm(ks[19], (L, D, D), D ** -0.5),
        'w_rg': nrm(ks[20], (L, D, N_GROUPS), D ** -0.5),
        'b_rg': nrm(ks[21], (L, N_GROUPS), 0.01),
        'w_re': nrm(ks[22], (L, D, N_EXPERTS), D ** -0.5),
        'b_re': nrm(ks[23], (L, N_EXPERTS), 0.01),
        'w_e_gate': nrm(ks[24], (L, N_EXPERTS, D, D_FF_E), D ** -0.5),
        'w_e_up': nrm(ks[25], (L, N_EXPERTS, D, D_FF_E), D ** -0.5),
        'w_e_down': nrm(ks[26], (L, N_EXPERTS, D_FF_E, D), D_FF_E ** -0.5),
    }


def reference(x_prompt, x_sample, cache_k, cache_v, cache_kidx, state_ret, c_prompt, c_sample,
              w_ada, b_ada, norm1_g, norm2_g, w_in, qn_g, kn_g, kidx_g, ret_norm_g, w_pa, w_pb,
              w_out, w_rg, b_rg, w_re, b_re, w_e_gate, w_e_up, w_e_down):
    yp, ys = x_prompt, x_sample
    kp, vp, kip, sp = [], [], [], []
    kss, vss, kis, sss = [], [], [], []
    bp = x_prompt.shape[0]
    for l in range(DEPTH):
        lw = (w_ada[l], b_ada[l], norm1_g[l], norm2_g[l], w_in[l], qn_g[l], kn_g[l], kidx_g[l],
              ret_norm_g[l], w_pa[l], w_pb[l], w_out[l], w_rg[l], b_rg[l], w_re[l], b_re[l],
              w_e_gate[l], w_e_up[l], w_e_down[l])
        empty_kv = jnp.zeros((bp, 0, N_HEADS_A, HEAD_DIM_A), yp.dtype)
        empty_ki = jnp.zeros((bp, 0, IDX_DIM), yp.dtype)
        s_zero = jnp.zeros((bp, N_HEADS_R, DK_R, DV_R), yp.dtype)
        yp, (k1, v1, ki1, s1) = layer(yp, c_prompt, empty_kv, empty_kv, empty_ki, s_zero, *lw)
        ys, (k2, v2, ki2, s2) = layer(ys, c_sample, cache_k[l], cache_v[l], cache_kidx[l], state_ret[l], *lw)
        kp.append(k1); vp.append(v1); kip.append(ki1); sp.append(s1)
        kss.append(k2); vss.append(v2); kis.append(ki2); sss.append(s2)
    return (yp, ys, jnp.stack(kp), jnp.stack(vp), jnp.stack(kip), jnp.stack(sp),
            jnp.stack(kss), jnp.stack(vss), jnp.stack(kis), jnp.stack(sss))
```

```python
import functools
import math

import jax
import jax.numpy as jnp
from jax import lax
from jax.experimental import pallas as pl
from jax.experimental.pallas import tpu as pltpu

F32 = jnp.float32
BF16 = jnp.bfloat16

D_MODEL = 1024
CHUNK = 64
N_HEADS_A = 8
HEAD_DIM_A = 64
IDX_HEADS = 4
IDX_DIM = 64
TOPK_MAX = 256
N_HEADS_R = 4
DK_R = 64
DV_R = 128
N_GROUPS = 4
EXPERTS_PER_GROUP = 4
N_EXPERTS = 16
D_FF_E = 512
EPS = 1e-6

WA = N_HEADS_A * HEAD_DIM_A
WQI = IDX_HEADS * IDX_DIM
WQR = N_HEADS_R * DK_R
WVR = N_HEADS_R * DV_R

LANES = 128
VMEM_LIMIT = 56 * 1024 * 1024

C_QA, C_KA, C_VA, C_QI, C_MISC = 0, 512, 1024, 1536, 1792
C_QR, C_KR, C_VR, C_GR, C_GA, C_GB, C_END = 1920, 2176, 2432, 2944, 3456, 4480, 5504
WI_LANE = IDX_DIM

INT_MIN = -2 ** 31
KEY_NEG_INF = -2139095041
NEG_BIG = -1e30


def _dot(a, b):
    return jnp.dot(a, b, preferred_element_type=F32)


def _dot_nt(a, b):
    return lax.dot_general(a, b, (((1,), (1,)), ((), ())), preferred_element_type=F32)


def _dot_tn(a, b):
    return lax.dot_general(a, b, (((0,), (0,)), ((), ())), preferred_element_type=F32)


def _rep(x, n):
    return x if n == LANES else jnp.concatenate([x] * (n // LANES), axis=1)


def _lane_chunk_sum(x):
    acc = x[:, 0:LANES]
    for c in range(1, x.shape[1] // LANES):
        acc = acc + x[:, c * LANES:(c + 1) * LANES]
    return acc


def _sigmoid(x):
    return 1.0 / (1.0 + jnp.exp(-x))


def _params(sem):
    return pltpu.CompilerParams(dimension_semantics=sem, vmem_limit_bytes=VMEM_LIMIT)


def _ada_kernel(c_ref, w_ref, b_ref, o_ref):
    c = c_ref[...]
    o_ref[...] = _dot(c * _sigmoid(c), w_ref[...]) + b_ref[...]


def _ada(c_all, w_ada, b_ada):
    rows = c_all.shape[0]
    ncol = w_ada.shape[1]
    bc = 1024
    return pl.pallas_call(
        _ada_kernel,
        grid=(ncol // bc,),
        in_specs=[pl.BlockSpec((rows, D_MODEL), lambda j: (0, 0)),
                  pl.BlockSpec((D_MODEL, bc), lambda j: (0, j)),
                  pl.BlockSpec((1, bc), lambda j: (0, j))],
        out_specs=pl.BlockSpec((rows, bc), lambda j: (0, j)),
        out_shape=jax.ShapeDtypeStruct((rows, ncol), F32),
        compiler_params=_params(("arbitrary",)),
        name="ada",
    )(c_all, w_ada, b_ada.reshape(1, ncol))


def _inproj_kernel(x_ref, mod_ref, n1g_ref, w_ref, bd_ref, qg_ref, kg_ref, kig_ref,
                   q_ref, kf_ref, kb_ref, vf_ref, vb_ref, qi_ref, kif_ref, kib_ref, wi_ref,
                   qr_ref, kr_ref, vr_ref, gr_ref, ga_ref, gb_ref):
    x = x_ref[0]
    sh1 = mod_ref[0, 0:1, :]
    sc1 = mod_ref[0, 1:2, :]
    h = x * lax.rsqrt(jnp.mean(x * x, axis=-1, keepdims=True) + EPS) * n1g_ref[...]
    hb = (h * (1.0 + sc1) + sh1).astype(BF16)

    def proj(c0, n):
        return _dot(hb, w_ref[:, c0:c0 + n])

    def headnorm(u, g):
        s = u * u
        s_hi = s.astype(BF16)
        s_lo = (s - s_hi.astype(F32)).astype(BF16)
        gs = _dot(s_hi, bd_ref[...]) + _dot(s_lo, bd_ref[...])
        return u * lax.rsqrt(gs * (1.0 / HEAD_DIM_A) + EPS) * g

    q_ref[0] = (headnorm(proj(C_QA, WA), qg_ref[...]) * (HEAD_DIM_A ** -0.5)).astype(BF16)
    kn = headnorm(proj(C_KA, WA), kg_ref[...])
    kf_ref[0] = kn
    kb_ref[0] = kn.astype(BF16)
    va = proj(C_VA, WA)
    vf_ref[0] = va
    vb_ref[0] = va.astype(BF16)
    qi_ref[0] = (proj(C_QI, WQI) * (IDX_DIM ** -0.5)).astype(BF16)

    misc = proj(C_MISC, LANES)
    lane = lax.broadcasted_iota(jnp.int32, misc.shape, 1)
    ss = jnp.sum(jnp.where(lane < IDX_DIM, misc * misc, 0.0), axis=-1, keepdims=True)
    kin = misc * lax.rsqrt(ss * (1.0 / IDX_DIM) + EPS) * kig_ref[...]
    kif_ref[0] = kin[:, 0:IDX_DIM]
    kib_ref[0] = (kin + pltpu.roll(kin, IDX_DIM, axis=1)).astype(BF16)
    wi_ref[0] = misc * (IDX_HEADS ** -0.5)

    qr_ref[0] = proj(C_QR, WQR).astype(BF16)
    kr_ref[0] = (proj(C_KR, WQR) * (DK_R ** -0.5)).astype(BF16)
    vr_ref[0] = proj(C_VR, WVR).astype(BF16)
    gr_ref[0] = proj(C_GR, WVR).astype(BF16)
    ga_ref[0] = proj(C_GA, D_MODEL).astype(BF16)
    gb_ref[0] = proj(C_GB, D_MODEL).astype(BF16)


def _inproj(x, mod, n1g, w_in_p, bd, qg, kg, kig, tm):
    B, T, _ = x.shape
    tok = lambda n: pl.BlockSpec((1, tm, n), lambda b, i: (b, i, 0))
    const = lambda r, c: pl.BlockSpec((r, c), lambda b, i: (0, 0))
    sd = lambda n, dt: jax.ShapeDtypeStruct((B, T, n), dt)
    outs = [(WA, BF16), (WA, F32), (WA, BF16), (WA, F32), (WA, BF16), (WQI, BF16),
            (IDX_DIM, F32), (LANES, BF16), (LANES, F32),
            (WQR, BF16), (WQR, BF16), (WVR, BF16), (WVR, BF16), (D_MODEL, BF16), (D_MODEL, BF16)]
    return pl.pallas_call(
        _inproj_kernel,
        grid=(B, T // tm),
        in_specs=[tok(D_MODEL), pl.BlockSpec((1, 6, D_MODEL), lambda b, i: (b, 0, 0)),
                  const(1, D_MODEL), const(D_MODEL, C_END), const(WA, WA),
                  const(1, WA), const(1, WA), const(1, LANES)],
        out_specs=[tok(n) for n, _ in outs],
        out_shape=[sd(n, dt) for n, dt in outs],
        compiler_params=_params(("parallel", "arbitrary")),
        name="inproj",
    )(x, mod, n1g, w_in_p, bd, qg, kg, kig)


def _dsa_kernel(q_ref, qi_ref, wi_ref, k_ref, v_ref, ki_ref, tri_ref, o_ref,
                key_ref, bias_ref, wib_ref, *, QB, TK, L, pos0, topk):
    p0 = pos0 + pl.program_id(1) * QB
    n_adm = jnp.minimum(((p0 + QB - 1) // CHUNK + 1) * CHUNK, L)
    n_tiles = (n_adm + TK - 1) // TK
    qpos = p0 + lax.broadcasted_iota(jnp.int32, (QB, TK), 0)
    lane_k = lax.broadcasted_iota(jnp.int32, (QB, TK), 1)
    half0 = lax.broadcasted_iota(jnp.int32, (QB, LANES), 1) < HEAD_DIM_A

    def head_half(pair, s):
        return jnp.where(half0 if s == 0 else jnp.logical_not(half0), pair, jnp.zeros_like(pair))

    qi = qi_ref[0]
    wi = wi_ref[0]
    qis = []
    for h in range(IDX_HEADS):
        qis.append(head_half(qi[:, LANES * (h // 2):LANES * (h // 2 + 1)], h % 2))
        wib_ref[h] = jnp.broadcast_to(wi[:, WI_LANE + h:WI_LANE + h + 1], (QB, LANES))

    def score_tile(kt, carry):
        r0 = pl.multiple_of(kt * TK, TK)
        kit = ki_ref[0, pl.ds(r0, TK), :]
        kpos = r0 + lane_k
        adm = jnp.logical_and((kpos // CHUNK) <= (qpos // CHUNK), kpos < L)
        s = jnp.zeros((QB, TK), F32)
        for h in range(IDX_HEADS):
            s = s + jnp.maximum(_dot_nt(qis[h], kit), 0.0) * _rep(wib_ref[h], TK)
        s = jnp.where(adm, s + 0.0, -jnp.inf)
        bits = lax.bitcast_convert_type(s, jnp.int32)
        key_ref[kt] = bits ^ ((bits >> 31) & 0x7FFFFFFF)
        return carry

    lax.fori_loop(0, n_tiles, score_tile, 0)

    def count(pred, thr):
        def body(kt, acc):
            key = key_ref[kt]
            for c in range(TK // LANES):
                acc = acc + jnp.where(pred(key[:, c * LANES:(c + 1) * LANES], thr), 1.0, 0.0)
            return acc
        acc = lax.fori_loop(0, n_tiles, body, jnp.zeros((QB, LANES), F32))
        return jnp.broadcast_to(jnp.sum(acc, axis=1, keepdims=True), (QB, LANES))

    def bisect(i, c):
        cand = c | jnp.left_shift(jnp.int32(1), 31 - i)
        cnt = count(lambda a, b: a >= b, cand ^ INT_MIN)
        return jnp.where(cnt >= float(topk), cand, c)

    thr = lax.fori_loop(0, 32, bisect, jnp.zeros((QB, LANES), jnp.int32)) ^ INT_MIN
    n_tie_take = float(topk) - count(lambda a, b: a > b, thr)

    def mask_tile(kt, n_eq_before):
        key = key_ref[kt]
        t = _rep(thr, TK)
        eq = key == t
        eqf = jnp.where(eq, 1.0, 0.0)
        rank = _dot(eqf.astype(BF16), tri_ref[...]) + _rep(n_eq_before, TK)
        sel = jnp.logical_or(key > t, jnp.logical_and(eq, rank < _rep(n_tie_take, TK)))
        sel = jnp.logical_and(sel, key > KEY_NEG_INF)
        bias_ref[kt] = jnp.where(sel, 0.0, NEG_BIG)
        return n_eq_before + jnp.broadcast_to(jnp.sum(eqf, axis=1, keepdims=True), (QB, LANES))

    lax.fori_loop(0, n_tiles, mask_tile, jnp.zeros((QB, LANES), F32))

    q = q_ref[0]
    qposf = qpos.astype(F32)
    for p in range(N_HEADS_A // 2):
        qpair = q[:, LANES * p:LANES * (p + 1)]
        outs = []
        for s in range(2):
            qh = head_half(qpair, s)
            slope = 2.0 ** (-8.0 * (2 * p + s + 1) / N_HEADS_A)

            def att_tile(kt, carry, qh=qh, slope=slope, p=p):
                m, l, acc = carry
                r0 = pl.multiple_of(kt * TK, TK)
                kp = k_ref[0, pl.ds(r0, TK), LANES * p:LANES * (p + 1)]
                vp = v_ref[0, pl.ds(r0, TK), LANES * p:LANES * (p + 1)]
                dist = jnp.abs(qposf - (r0 + lane_k).astype(F32))
                lg = _dot_nt(qh, kp) + bias_ref[kt] - slope * dist
                mx = jnp.broadcast_to(jnp.max(lg, axis=1, keepdims=True), (QB, LANES))
                m_new = jnp.maximum(m, mx)
                alpha = jnp.exp(m - m_new)
                pe = jnp.exp(lg - _rep(m_new, TK))
                l = l * alpha + _lane_chunk_sum(pe)
                acc = acc * alpha + _dot(pe.astype(BF16), vp)
                return m_new, l, acc

            init = (jnp.full((QB, LANES), -3e38, F32), jnp.zeros((QB, LANES), F32),
                    jnp.zeros((QB, LANES), F32))
            _, l, acc = lax.fori_loop(0, n_tiles, att_tile, init)
            outs.append(acc / jnp.broadcast_to(jnp.sum(l, axis=1, keepdims=True), (QB, LANES)))
        o_ref[0, :, LANES * p:LANES * (p + 1)] = jnp.where(half0, outs[0], outs[1]).astype(BF16)


def _dsa(q, qi, wi, k, v, ki, tri, *, L, pos0, QB, TK):
    B, T, _ = q.shape
    Lp = k.shape[1]
    topk = min(TOPK_MAX, L // 4)
    blk = lambda n: pl.BlockSpec((1, QB, n), lambda b, i: (b, i, 0))
    full = lambda n: pl.BlockSpec((1, Lp, n), lambda b, i: (b, 0, 0))
    return pl.pallas_call(
        functools.partial(_dsa_kernel, QB=QB, TK=TK, L=L, pos0=pos0, topk=topk),
        grid=(B, T // QB),
        in_specs=[blk(WA), blk(WQI), blk(LANES), full(WA), full(WA), full(LANES),
                  pl.BlockSpec((TK, TK), lambda b, i: (0, 0))],
        out_specs=blk(WA),
        out_shape=jax.ShapeDtypeStruct((B, T, WA), BF16),
        scratch_shapes=[pltpu.VMEM((Lp // TK, QB, TK), jnp.int32),
                        pltpu.VMEM((Lp // TK, QB, TK), F32),
                        pltpu.VMEM((IDX_HEADS, QB, LANES), F32)],
        compiler_params=_params(("parallel", "arbitrary")),
        name="dsa",
    )(q, qi, wi, k, v, ki, tri)


def _ret_kernel(q_ref, k_ref, v_ref, s0_ref, o_ref, sfin_ref, s_ref, *, C):
    @pl.when(pl.program_id(1) == 0)
    def _():
        s_ref[...] = s0_ref[0]

    q = q_ref[0]
    k = k_ref[0]
    v = v_ref[0]
    half0 = lax.broadcasted_iota(jnp.int32, (C, LANES), 1) < DK_R
    row = lax.broadcasted_iota(jnp.int32, (C, LANES), 0).astype(F32)
    rel = (lax.broadcasted_iota(jnp.int32, (C, C), 0) - lax.broadcasted_iota(jnp.int32, (C, C), 1)).astype(F32)
    zeros = jnp.zeros((DK_R, DV_R), F32)
    for h in range(N_HEADS_R):
        lg = math.log1p(-(2.0 ** (-5.0 - h)))
        p, s = h // 2, h % 2
        mine = half0 if s == 0 else jnp.logical_not(half0)
        qpair = q[:, LANES * p:LANES * (p + 1)]
        kpair = k[:, LANES * p:LANES * (p + 1)]
        qh = jnp.where(mine, qpair, jnp.zeros_like(qpair))
        vh = v[:, DV_R * h:DV_R * (h + 1)]
        decay = jnp.where(rel >= 0.0, jnp.exp(jnp.maximum(rel, 0.0) * lg), 0.0)
        inner = _dot_nt(qh, kpair) * decay
        sh = s_ref[h]
        spad = jnp.concatenate([sh, zeros] if s == 0 else [zeros, sh], axis=0).astype(BF16)
        o = _dot(inner.astype(BF16), vh) + _dot(qh, spad) * jnp.exp((row + 1.0) * lg)
        o_ref[0, :, DV_R * h:DV_R * (h + 1)] = o.astype(BF16)
        kd = (kpair.astype(F32) * jnp.exp((C - 1.0 - row) * lg)).astype(BF16)
        upd = _dot_tn(kd, vh)
        s_ref[h] = sh * math.exp(C * lg) + upd[DK_R * s:DK_R * (s + 1), :]
    sfin_ref[0] = s_ref[...]


def _retention(qr, kr, vr, s0, C):
    B, T, _ = qr.shape
    blk = lambda n: pl.BlockSpec((1, C, n), lambda b, c: (b, c, 0))
    st = pl.BlockSpec((1, N_HEADS_R, DK_R, DV_R), lambda b, c: (b, 0, 0, 0))
    return pl.pallas_call(
        functools.partial(_ret_kernel, C=C),
        grid=(B, T // C),
        in_specs=[blk(WQR), blk(WQR), blk(WVR), st],
        out_specs=[blk(WVR), st],
        out_shape=[jax.ShapeDtypeStruct((B, T, WVR), BF16),
                   jax.ShapeDtypeStruct((B, N_HEADS_R, DK_R, DV_R), F32)],
        scratch_shapes=[pltpu.VMEM((N_HEADS_R, DK_R, DV_R), F32)],
        compiler_params=_params(("parallel", "arbitrary")),
        name="retention",
    )(qr, kr, vr, s0)


def _merge_kernel(x_ref, mod_ref, oa_ref, orr_ref, gr_ref, ga_ref, gb_ref, rg_ref, wpa_ref, wpb_ref,
                  wout_ref, n2g_ref, wrh_ref, wrl_ref, br_ref, x1_ref, h2_ref, comb_ref):
    orr = orr_ref[0].astype(F32)
    parts = []
    for h in range(N_HEADS_R):
        oh = orr[:, DV_R * h:DV_R * (h + 1)]
        parts.append(oh * lax.rsqrt(jnp.mean(oh * oh, axis=-1, keepdims=True) + EPS))
    gr = gr_ref[0].astype(F32)
    on = jnp.concatenate(parts, axis=1) * rg_ref[...] * (gr * _sigmoid(gr))
    a = _dot(oa_ref[0], wpa_ref[...])
    b = _dot(on.astype(BF16), wpb_ref[...])
    merged = _sigmoid(ga_ref[0].astype(F32)) * a + _sigmoid(gb_ref[0].astype(F32)) * b
    y = _dot(merged.astype(BF16), wout_ref[...])
    x1 = x_ref[0] + mod_ref[0, 2:3, :] * y
    x1_ref[0] = x1
    h2 = x1 * lax.rsqrt(jnp.mean(x1 * x1, axis=-1, keepdims=True) + EPS) * n2g_ref[...]
    h2 = h2 * (1.0 + mod_ref[0, 4:5, :]) + mod_ref[0, 3:4, :]
    h2_ref[0] = h2.astype(BF16)

    h_hi = h2.astype(BF16)
    h_lo = (h2 - h_hi.astype(F32)).astype(BF16)
    lgt = _dot(h_hi, wrh_ref[...]) + _dot(h_hi, wrl_ref[...]) + _dot(h_lo, wrh_ref[...]) + br_ref[...]
    lane = lax.broadcasted_iota(jnp.int32, lgt.shape, 1)
    big = jnp.int32(1 << 20)

    def rmax(x):
        return jnp.max(x, axis=-1, keepdims=True)

    def first_lane(m):
        return jnp.min(jnp.where(m, lane, big), axis=-1, keepdims=True)

    gmask = lane < N_GROUPS
    gl = jnp.where(gmask, lgt, -jnp.inf)
    gmax = rmax(gl)
    g_top_p = 1.0 / jnp.sum(jnp.exp(gl - gmax), axis=-1, keepdims=True)
    g_idx = first_lane(jnp.logical_and(gmask, gl == gmax))
    el = lane - N_GROUPS
    emask = jnp.logical_and(jnp.logical_and(el >= 0, el < N_EXPERTS), (el // EXPERTS_PER_GROUP) == g_idx)
    e1 = jnp.where(emask, lgt, -jnp.inf)
    l1 = rmax(e1)
    i1 = first_lane(jnp.logical_and(emask, e1 == l1))
    e2 = jnp.where(lane == i1, -jnp.inf, e1)
    l2 = rmax(e2)
    i2 = first_lane(jnp.logical_and(emask, jnp.logical_and(e2 == l2, lane != i1)))
    r = jnp.exp(l2 - l1)
    w1 = g_top_p / (1.0 + r)
    comb_ref[0] = jnp.where(lane == i1, w1, jnp.where(lane == i2, w1 * r, 0.0))


def _merge(x, mod, oa, orr, gr, ga, gb, rg, wpa, wpb, wout, n2g, wrh, wrl, br, tm):
    B, T, _ = x.shape
    tok = lambda n: pl.BlockSpec((1, tm, n), lambda b, i: (b, i, 0))
    const = lambda r, c: pl.BlockSpec((r, c), lambda b, i: (0, 0))
    return pl.pallas_call(
        _merge_kernel,
        grid=(B, T // tm),
        in_specs=[tok(D_MODEL), pl.BlockSpec((1, 6, D_MODEL), lambda b, i: (b, 0, 0)),
                  tok(WA), tok(WVR), tok(WVR), tok(D_MODEL), tok(D_MODEL),
                  const(1, WVR), const(WA, D_MODEL), const(WVR, D_MODEL), const(D_MODEL, D_MODEL),
                  const(1, D_MODEL), const(D_MODEL, LANES), const(D_MODEL, LANES), const(1, LANES)],
        out_specs=[tok(D_MODEL), tok(D_MODEL), tok(LANES)],
        out_shape=[jax.ShapeDtypeStruct((B, T, D_MODEL), F32),
                   jax.ShapeDtypeStruct((B, T, D_MODEL), BF16),
                   jax.ShapeDtypeStruct((B, T, LANES), F32)],
        compiler_params=_params(("parallel", "arbitrary")),
        name="merge",
    )(x, mod, oa, orr, gr, ga, gb, rg, wpa, wpb, wout, n2g, wrh, wrl, br)


def _moe_kernel(h2_ref, comb_ref, x1_ref, mod_ref, wg_ref, wu_ref, wd_ref, y_ref, acc_ref):
    e = pl.program_id(2)

    @pl.when(e == 0)
    def _():
        acc_ref[...] = jnp.zeros_like(acc_ref)

    hb = h2_ref[0]
    g = _dot(hb, wg_ref[0])
    he = (g * _sigmoid(g)) * _dot(hb, wu_ref[0])
    comb = comb_ref[0]
    lane = lax.broadcasted_iota(jnp.int32, comb.shape, 1)
    ce = jnp.sum(jnp.where(lane == e + N_GROUPS, comb, 0.0), axis=-1, keepdims=True)
    acc_ref[...] += ce * _dot(he.astype(BF16), wd_ref[0])

    @pl.when(e == N_EXPERTS - 1)
    def _():
        y_ref[0] = x1_ref[0] + mod_ref[0, 5:6, :] * acc_ref[...]


def _moe(h2, comb, x1, mod, wg, wu, wd, tm):
    B, T, _ = h2.shape
    tok = lambda n: pl.BlockSpec((1, tm, n), lambda b, i, e: (b, i, 0))
    return pl.pallas_call(
        _moe_kernel,
        grid=(B, T // tm, N_EXPERTS),
        in_specs=[tok(D_MODEL), tok(LANES), tok(D_MODEL),
                  pl.BlockSpec((1, 6, D_MODEL), lambda b, i, e: (b, 0, 0)),
                  pl.BlockSpec((1, D_MODEL, D_FF_E), lambda b, i, e: (e, 0, 0)),
                  pl.BlockSpec((1, D_MODEL, D_FF_E), lambda b, i, e: (e, 0, 0)),
                  pl.BlockSpec((1, D_FF_E, D_MODEL), lambda b, i, e: (e, 0, 0))],
        out_specs=tok(D_MODEL),
        out_shape=jax.ShapeDtypeStruct((B, T, D_MODEL), F32),
        scratch_shapes=[pltpu.VMEM((tm, D_MODEL), F32)],
        compiler_params=_params(("parallel", "parallel", "arbitrary")),
        name="moe",
    )(h2, comb, x1, mod, wg, wu, wd)


def _pad_rows(a, n):
    return a if a.shape[1] == n else jnp.pad(a, ((0, 0), (0, n - a.shape[1]), (0, 0)))


def _group_layer(x, mod, past_k, past_v, past_ki, s0, w, *, tm, QB, TK, C):
    B, T, _ = x.shape
    P = past_k.shape[1]
    L = P + T
    (q, kf, kb, vf, vb, qi, kif, kib, wi, qr, kr, vr, gr, ga, gb) = _inproj(
        x, mod, w["n1g"], w["w_in"], w["bd"], w["qg"], w["kg"], w["kig"], tm)
    Lp = -(-L // TK) * TK
    if P:
        pk = past_k.reshape(B, P, WA).astype(BF16)
        pv = past_v.reshape(B, P, WA).astype(BF16)
        pki = past_ki.astype(BF16)
        k_all = jnp.concatenate([pk, kb], axis=1)
        v_all = jnp.concatenate([pv, vb], axis=1)
        ki_all = jnp.concatenate([jnp.concatenate([pki, pki], axis=-1), kib], axis=1)
    else:
        k_all, v_all, ki_all = kb, vb, kib
    oa = _dsa(q, qi, wi, _pad_rows(k_all, Lp), _pad_rows(v_all, Lp), _pad_rows(ki_all, Lp), w["tri"],
              L=L, pos0=P, QB=QB, TK=TK)
    orr, s_new = _retention(qr, kr, vr, s0, C)
    x1, h2, comb = _merge(x, mod, oa, orr, gr, ga, gb, w["rg"], w["wpa"], w["wpb"], w["wout"],
                          w["n2g"], w["wrh"], w["wrl"], w["br"], tm)
    y = _moe(h2, comb, x1, mod, w["wg"], w["wu"], w["wd"], tm)
    return y, (kf.reshape(B, T, N_HEADS_A, HEAD_DIM_A), vf.reshape(B, T, N_HEADS_A, HEAD_DIM_A), kif, s_new)


def _pack_w_in(w_in):
    o = 0
    parts = []
    for n in (WA, WA, WA, WQI):
        parts.append(w_in[:, o:o + n])
        o += n
    parts.append(w_in[:, o:o + IDX_DIM + IDX_HEADS])
    o += IDX_DIM + IDX_HEADS
    parts.append(jnp.zeros((D_MODEL, LANES - IDX_DIM - IDX_HEADS), w_in.dtype))
    parts.append(w_in[:, o:])
    return jnp.concatenate(parts, axis=1).astype(BF16)


def kernel(x_prompt, x_sample, cache_k, cache_v, cache_kidx, state_ret, c_prompt, c_sample, w_ada, b_ada,
           norm1_g, norm2_g, w_in, qn_g, kn_g, kidx_g, ret_norm_g, w_pa, w_pb, w_out, w_rg, b_rg, w_re, b_re,
           w_e_gate, w_e_up, w_e_down):
    depth = w_ada.shape[0]
    bp, bs = x_prompt.shape[0], x_sample.shape[0]
    TK = 256
    gid = jnp.arange(WA, dtype=jnp.int32) // HEAD_DIM_A
    bd = (gid[:, None] == gid[None, :]).astype(BF16)
    kk = jnp.arange(TK, dtype=jnp.int32)
    tri = (kk[:, None] < kk[None, :]).astype(BF16)

    yp, ys = x_prompt, x_sample
    outs = [[] for _ in range(8)]
    for l in range(depth):
        wr = jnp.concatenate([w_rg[l], w_re[l],
                              jnp.zeros((D_MODEL, LANES - N_GROUPS - N_EXPERTS), F32)], axis=1)
        wrh = wr.astype(BF16)
        w = dict(
            n1g=norm1_g[l].reshape(1, D_MODEL), n2g=norm2_g[l].reshape(1, D_MODEL),
            w_in=_pack_w_in(w_in[l]), bd=bd, tri=tri,
            qg=jnp.tile(qn_g[l], N_HEADS_A).reshape(1, WA), kg=jnp.tile(kn_g[l], N_HEADS_A).reshape(1, WA),
            kig=jnp.concatenate([kidx_g[l], jnp.zeros((LANES - IDX_DIM,), F32)]).reshape(1, LANES),
            rg=jnp.tile(ret_norm_g[l], N_HEADS_R).reshape(1, WVR),
            wpa=w_pa[l].astype(BF16), wpb=w_pb[l].astype(BF16), wout=w_out[l].astype(BF16),
            wrh=wrh, wrl=(wr - wrh.astype(F32)).astype(BF16),
            br=jnp.concatenate([b_rg[l], b_re[l], jnp.zeros((LANES - N_GROUPS - N_EXPERTS,), F32)]).reshape(1, LANES),
            wg=w_e_gate[l].astype(BF16), wu=w_e_up[l].astype(BF16), wd=w_e_down[l].astype(BF16),
        )
        rows = -(-(bp + bs) // 8) * 8
        c_all = jnp.concatenate([c_prompt, c_sample, jnp.zeros((rows - bp - bs, D_MODEL), F32)], axis=0)
        mod = _ada(c_all, w_ada[l], b_ada[l]).reshape(rows, 6, D_MODEL)
        tp, ts = yp.shape[1], ys.shape[1]
        empty_kv = jnp.zeros((bp, 0, N_HEADS_A, HEAD_DIM_A), F32)
        empty_ki = jnp.zeros((bp, 0, IDX_DIM), F32)
        s_zero = jnp.zeros((bp, N_HEADS_R, DK_R, DV_R), F32)
        yp, st_p = _group_layer(yp, mod[:bp], empty_kv, empty_kv, empty_ki, s_zero, w,
                                tm=min(512, tp), QB=min(128, tp), TK=TK, C=min(256, tp))
        ys, st_s = _group_layer(ys, mod[bp:bp + bs], cache_k[l], cache_v[l], cache_kidx[l], state_ret[l], w,
                                tm=min(512, ts), QB=min(128, ts), TK=TK, C=min(256, ts))
        for i, a in enumerate(st_p + st_s):
            outs[i].append(a)
    return (yp, ys) + tuple(jnp.stack(o) for o in outs)
```

```python
import functools
import math

import jax
import jax.numpy as jnp
from jax import lax
from jax.experimental import pallas as pl
from jax.experimental.pallas import tpu as pltpu

F32 = jnp.float32
BF16 = jnp.bfloat16

D_MODEL = 1024
CHUNK = 64
CHUNK_SHIFT = 6
N_HEADS_A = 8
HEAD_DIM_A = 64
IDX_HEADS = 4
IDX_DIM = 64
TOPK_MAX = 256
N_HEADS_R = 4
DK_R = 64
DV_R = 128
N_GROUPS = 4
EXPERTS_PER_GROUP = 4
N_EXPERTS = 16
D_FF_E = 512
EPS = 1e-6

WA = N_HEADS_A * HEAD_DIM_A
WQI = IDX_HEADS * IDX_DIM
WQR = N_HEADS_R * DK_R
WVR = N_HEADS_R * DV_R

LANES = 128
VMEM_LIMIT = 56 * 1024 * 1024

C_QA, C_KA, C_VA, C_QI, C_MISC = 0, 512, 1024, 1536, 1792
C_QR, C_KR, C_VR, C_GR, C_GA, C_GB, C_END = 1920, 2176, 2432, 2944, 3456, 4480, 5504
WI_LANE = IDX_DIM

INT_MIN = -2 ** 31
KEY_NEG_INF = -2139095041
NEG_BIG = -1e30
LOG2E = 1.4426950408889634
ALIBI_RATIO = 2.0 ** (-8.0 / N_HEADS_A)


def _dot(a, b):
    return jnp.dot(a, b, preferred_element_type=F32)


def _dot_nt(a, b):
    return lax.dot_general(a, b, (((1,), (1,)), ((), ())), preferred_element_type=F32)


def _dot_tn(a, b):
    return lax.dot_general(a, b, (((0,), (0,)), ((), ())), preferred_element_type=F32)


def _rep(x, n):
    return x if n == LANES else jnp.concatenate([x] * (n // LANES), axis=1)


def _lane_chunk_sum(x):
    acc = x[:, 0:LANES]
    for c in range(1, x.shape[1] // LANES):
        acc = acc + x[:, c * LANES:(c + 1) * LANES]
    return acc


def _sigmoid(x):
    return 1.0 / (1.0 + jnp.exp(-x))


def _params(sem):
    return pltpu.CompilerParams(dimension_semantics=sem, vmem_limit_bytes=VMEM_LIMIT)


def _ada_kernel(c_ref, w_ref, b_ref, o_ref):
    c = c_ref[...]
    o_ref[...] = _dot(c * _sigmoid(c), w_ref[...]) + b_ref[...]


def _ada(c_all, w_ada, b_ada):
    rows = c_all.shape[0]
    ncol = w_ada.shape[1]
    bc = 1024
    return pl.pallas_call(
        _ada_kernel,
        grid=(ncol // bc,),
        in_specs=[pl.BlockSpec((rows, D_MODEL), lambda j: (0, 0)),
                  pl.BlockSpec((D_MODEL, bc), lambda j: (0, j)),
                  pl.BlockSpec((1, bc), lambda j: (0, j))],
        out_specs=pl.BlockSpec((rows, bc), lambda j: (0, j)),
        out_shape=jax.ShapeDtypeStruct((rows, ncol), F32),
        compiler_params=_params(("arbitrary",)),
        name="ada",
    )(c_all, w_ada, b_ada.reshape(1, ncol))


def _inproj_kernel(x_ref, mod_ref, n1g_ref, w_ref, bd_ref, qg_ref, kg_ref, kig_ref,
                   q_ref, kf_ref, kb_ref, vf_ref, vb_ref, qi_ref, kif_ref, kib_ref, wi_ref,
                   qr_ref, kr_ref, vr_ref, gr_ref, ga_ref, gb_ref, *, keys_transposed):
    x = x_ref[0]
    sh1 = mod_ref[0, 0:1, :]
    sc1 = mod_ref[0, 1:2, :]
    h = x * lax.rsqrt(jnp.mean(x * x, axis=-1, keepdims=True) + EPS) * n1g_ref[...]
    hb = (h * (1.0 + sc1) + sh1).astype(BF16)

    def proj(c0, n):
        return _dot(hb, w_ref[:, c0:c0 + n])

    def headnorm(u, g):
        s = u * u
        s_hi = s.astype(BF16)
        s_lo = (s - s_hi.astype(F32)).astype(BF16)
        gs = _dot(s_hi, bd_ref[...]) + _dot(s_lo, bd_ref[...])
        return u * lax.rsqrt(gs * (1.0 / HEAD_DIM_A) + EPS) * g

    q_ref[0] = (headnorm(proj(C_QA, WA), qg_ref[...]) * (LOG2E * HEAD_DIM_A ** -0.5)).astype(BF16)
    kn = headnorm(proj(C_KA, WA), kg_ref[...])
    kf_ref[0] = kn
    if keys_transposed:
        kb_ref[0, 0] = kn.T.astype(BF16)
    else:
        kb_ref[0] = kn.astype(BF16)
    va = proj(C_VA, WA)
    vf_ref[0] = va
    vb_ref[0] = va.astype(BF16)
    qi_ref[0] = (proj(C_QI, WQI) * (IDX_DIM ** -0.5)).astype(BF16)

    misc = proj(C_MISC, LANES)
    lane = lax.broadcasted_iota(jnp.int32, misc.shape, 1)
    ss = jnp.sum(jnp.where(lane < IDX_DIM, misc * misc, 0.0), axis=-1, keepdims=True)
    kin = misc * lax.rsqrt(ss * (1.0 / IDX_DIM) + EPS) * kig_ref[...]
    kif_ref[0] = kin[:, 0:IDX_DIM]
    kdup = kin + pltpu.roll(kin, IDX_DIM, axis=1)
    if keys_transposed:
        kib_ref[0, 0] = kdup.T.astype(BF16)
    else:
        kib_ref[0] = kdup.astype(BF16)
    wi_ref[0] = misc * (IDX_HEADS ** -0.5)

    qr_ref[0] = proj(C_QR, WQR).astype(BF16)
    kr_ref[0] = (proj(C_KR, WQR) * (DK_R ** -0.5)).astype(BF16)
    vr_ref[0] = proj(C_VR, WVR).astype(BF16)
    gr_ref[0] = proj(C_GR, WVR).astype(BF16)
    ga_ref[0] = proj(C_GA, D_MODEL).astype(BF16)
    gb_ref[0] = proj(C_GB, D_MODEL).astype(BF16)


def _inproj(x, mod, n1g, w_in_p, bd, qg, kg, kig, tm, keys_transposed):
    B, T, _ = x.shape
    tok = lambda n: pl.BlockSpec((1, tm, n), lambda b, i: (b, i, 0))
    const = lambda r, c: pl.BlockSpec((r, c), lambda b, i: (0, 0))
    sd = lambda n, dt: jax.ShapeDtypeStruct((B, T, n), dt)
    outs = [(WA, BF16), (WA, F32), (WA, BF16), (WA, F32), (WA, BF16), (WQI, BF16),
            (IDX_DIM, F32), (LANES, BF16), (LANES, F32),
            (WQR, BF16), (WQR, BF16), (WVR, BF16), (WVR, BF16), (D_MODEL, BF16), (D_MODEL, BF16)]
    out_specs = [tok(n) for n, _ in outs]
    out_shape = [sd(n, dt) for n, dt in outs]
    if keys_transposed:
        for idx in (2, 7):
            n = outs[idx][0]
            out_specs[idx] = pl.BlockSpec((1, 1, n, tm), lambda b, i: (b, i, 0, 0))
            out_shape[idx] = jax.ShapeDtypeStruct((B, T // tm, n, tm), BF16)
    return pl.pallas_call(
        functools.partial(_inproj_kernel, keys_transposed=keys_transposed),
        grid=(B, T // tm),
        in_specs=[tok(D_MODEL), pl.BlockSpec((1, 6, D_MODEL), lambda b, i: (b, 0, 0)),
                  const(1, D_MODEL), const(D_MODEL, C_END), const(WA, WA),
                  const(1, WA), const(1, WA), const(1, LANES)],
        out_specs=out_specs,
        out_shape=out_shape,
        compiler_params=_params(("parallel", "arbitrary")),
        name="inproj",
    )(x, mod, n1g, w_in_p, bd, qg, kg, kig)


def _dsa_kernel(q_ref, qi_ref, wi_ref, kt_ref, v_ref, kit_ref, tri_ref, o_ref,
                key_ref, wib_ref, qih_ref, qh_ref, m_ref, l_ref, acc_ref, *, QB, TK, L, pos0, topk):
    p0 = pos0 + pl.program_id(1) * QB
    n_adm = jnp.minimum(((p0 + QB - 1) // CHUNK + 1) * CHUNK, L)
    n_tiles = (n_adm + TK - 1) // TK
    qpos = p0 + lax.broadcasted_iota(jnp.int32, (QB, TK), 0)
    lane_k = lax.broadcasted_iota(jnp.int32, (QB, TK), 1)
    half0 = lax.broadcasted_iota(jnp.int32, (QB, LANES), 1) < HEAD_DIM_A

    def head_half(pair, s):
        return jnp.where(half0 if s == 0 else jnp.logical_not(half0), pair, jnp.zeros_like(pair))

    qi = qi_ref[0]
    wi = wi_ref[0]
    for h in range(IDX_HEADS):
        qih_ref[h] = head_half(qi[:, LANES * (h // 2):LANES * (h // 2 + 1)], h % 2)
        wib_ref[h] = jnp.broadcast_to(wi[:, WI_LANE + h:WI_LANE + h + 1], (QB, LANES))

    def score_tile(kt, carry):
        kit = kit_ref[0, kt]
        kpos = kt * TK + lane_k
        adm = jnp.logical_and((kpos >> CHUNK_SHIFT) <= (qpos >> CHUNK_SHIFT), kpos < L)
        s = jnp.zeros((QB, TK), F32)
        for h in range(IDX_HEADS):
            s = s + jnp.maximum(_dot(qih_ref[h], kit), 0.0) * _rep(wib_ref[h], TK)
        s = jnp.where(adm, s + 0.0, -jnp.inf)
        bits = lax.bitcast_convert_type(s, jnp.int32)
        key_ref[kt] = bits ^ ((bits >> 31) & 0x7FFFFFFF)
        return carry

    lax.fori_loop(0, n_tiles, score_tile, 0)

    RB = min(QB, 128)

    def count(pred, thr):
        parts = []
        for r in range(QB // RB):
            thr_r = thr[r * RB:(r + 1) * RB]

            def body(kt, acc, r=r, thr_r=thr_r):
                for c in range(TK // LANES):
                    hit = pred(key_ref[kt, r * RB:(r + 1) * RB, c * LANES:(c + 1) * LANES], thr_r)
                    acc = acc + jnp.where(hit, 1.0, 0.0)
                return acc
            acc = lax.fori_loop(0, n_tiles, body, jnp.zeros((RB, LANES), F32))
            parts.append(jnp.broadcast_to(jnp.sum(acc, axis=1, keepdims=True), (RB, LANES)))
        return parts[0] if len(parts) == 1 else jnp.concatenate(parts, axis=0)

    def bisect(i, thr):
        cand = thr + jnp.left_shift(jnp.int32(1), 31 - i)
        cnt = count(lambda a, b: a >= b, cand)
        return jnp.where(cnt >= float(topk), cand, thr)

    thr = lax.fori_loop(0, 32, bisect, jnp.full((QB, LANES), INT_MIN, jnp.int32))
    n_tie_take = float(topk) - count(lambda a, b: a > b, thr)

    def mask_tile(kt, n_eq_before):
        key = key_ref[kt]
        t = _rep(thr, TK)
        eq = key == t
        eqf = jnp.where(eq, 1.0, 0.0)
        rank = _dot(eqf.astype(BF16), tri_ref[...]) + _rep(n_eq_before, TK)
        sel = jnp.logical_or(key > t, jnp.logical_and(eq, rank < _rep(n_tie_take, TK)))
        sel = jnp.logical_and(sel, key > KEY_NEG_INF)
        key_ref[kt] = lax.bitcast_convert_type(jnp.where(sel, 0.0, NEG_BIG), jnp.int32)
        return n_eq_before + jnp.broadcast_to(jnp.sum(eqf, axis=1, keepdims=True), (QB, LANES))

    lax.fori_loop(0, n_tiles, mask_tile, jnp.zeros((QB, LANES), F32))

    q = q_ref[0]
    for h in range(N_HEADS_A):
        qh_ref[h] = head_half(q[:, LANES * (h // 2):LANES * (h // 2 + 1)], h % 2)
    m_ref[...] = jnp.full(m_ref.shape, -3e38, F32)
    l_ref[...] = jnp.zeros(l_ref.shape, F32)
    acc_ref[...] = jnp.zeros(acc_ref.shape, F32)
    qposf = qpos.astype(F32)

    def att_tile(kt, carry):
        r0 = pl.multiple_of(kt * TK, TK)
        dist = jnp.abs(qposf - (r0 + lane_k).astype(F32))
        t = lax.bitcast_convert_type(key_ref[kt], F32) - (ALIBI_RATIO * LOG2E) * dist
        for p in range(N_HEADS_A // 2):
            kp = kt_ref[0, kt, LANES * p:LANES * (p + 1), :]
            vp = v_ref[0, pl.ds(r0, TK), LANES * p:LANES * (p + 1)]
            for s in range(2):
                h = 2 * p + s
                lg = _dot(qh_ref[h], kp) + t
                if h + 1 < N_HEADS_A:
                    t = ALIBI_RATIO * t
                m = m_ref[h]
                m_new = jnp.maximum(m, jnp.broadcast_to(jnp.max(lg, axis=1, keepdims=True), (QB, LANES)))
                alpha = jnp.exp2(m - m_new)
                pe = jnp.exp2(lg - _rep(m_new, TK))
                m_ref[h] = m_new
                l_ref[h] = l_ref[h] * alpha + _lane_chunk_sum(pe)
                acc_ref[h] = acc_ref[h] * alpha + _dot(pe.astype(BF16), vp)
        return carry

    lax.fori_loop(0, n_tiles, att_tile, 0)

    for p in range(N_HEADS_A // 2):
        outs = []
        for s in range(2):
            h = 2 * p + s
            outs.append(acc_ref[h] / jnp.broadcast_to(jnp.sum(l_ref[h], axis=1, keepdims=True), (QB, LANES)))
        o_ref[0, :, LANES * p:LANES * (p + 1)] = jnp.where(half0, outs[0], outs[1]).astype(BF16)


def _dsa(q, qi, wi, kt, v, kit, tri, *, L, pos0, QB):
    B, T, _ = q.shape
    _, nkt, _, TK = kt.shape
    Lp = nkt * TK
    topk = min(TOPK_MAX, L // 4)
    blk = lambda n: pl.BlockSpec((1, QB, n), lambda b, i: (b, i, 0))
    once = pl.Buffered(1)
    return pl.pallas_call(
        functools.partial(_dsa_kernel, QB=QB, TK=TK, L=L, pos0=pos0, topk=topk),
        grid=(B, T // QB),
        in_specs=[blk(WA), blk(WQI), blk(LANES),
                  pl.BlockSpec((1, nkt, WA, TK), lambda b, i: (b, 0, 0, 0), pipeline_mode=once),
                  pl.BlockSpec((1, Lp, WA), lambda b, i: (b, 0, 0), pipeline_mode=once),
                  pl.BlockSpec((1, nkt, LANES, TK), lambda b, i: (b, 0, 0, 0), pipeline_mode=once),
                  pl.BlockSpec((TK, TK), lambda b, i: (0, 0), pipeline_mode=once)],
        out_specs=blk(WA),
        out_shape=jax.ShapeDtypeStruct((B, T, WA), BF16),
        scratch_shapes=[pltpu.VMEM((nkt, QB, TK), jnp.int32),
                        pltpu.VMEM((IDX_HEADS, QB, LANES), F32),
                        pltpu.VMEM((IDX_HEADS, QB, LANES), BF16),
                        pltpu.VMEM((N_HEADS_A, QB, LANES), BF16),
                        pltpu.VMEM((N_HEADS_A, QB, LANES), F32),
                        pltpu.VMEM((N_HEADS_A, QB, LANES), F32),
                        pltpu.VMEM((N_HEADS_A, QB, LANES), F32)],
        compiler_params=_params(("parallel", "arbitrary")),
        name="dsa",
    )(q, qi, wi, kt, v, kit, tri)


def _ret_kernel(q_ref, k_ref, v_ref, s0_ref, o_ref, sfin_ref, s_ref, *, C):
    @pl.when(pl.program_id(1) == 0)
    def _():
        s_ref[...] = s0_ref[0]

    q = q_ref[0]
    k = k_ref[0]
    v = v_ref[0]
    half0 = lax.broadcasted_iota(jnp.int32, (C, LANES), 1) < DK_R
    row = lax.broadcasted_iota(jnp.int32, (C, LANES), 0).astype(F32)
    rel = (lax.broadcasted_iota(jnp.int32, (C, C), 0) - lax.broadcasted_iota(jnp.int32, (C, C), 1)).astype(F32)
    zeros = jnp.zeros((DK_R, DV_R), F32)
    for h in range(N_HEADS_R):
        lg = math.log1p(-(2.0 ** (-5.0 - h)))
        p, s = h // 2, h % 2
        mine = half0 if s == 0 else jnp.logical_not(half0)
        qpair = q[:, LANES * p:LANES * (p + 1)]
        kpair = k[:, LANES * p:LANES * (p + 1)]
        qh = jnp.where(mine, qpair, jnp.zeros_like(qpair))
        vh = v[:, DV_R * h:DV_R * (h + 1)]
        decay = jnp.where(rel >= 0.0, jnp.exp(jnp.maximum(rel, 0.0) * lg), 0.0)
        inner = _dot_nt(qh, kpair) * decay
        sh = s_ref[h]
        spad = jnp.concatenate([sh, zeros] if s == 0 else [zeros, sh], axis=0).astype(BF16)
        o = _dot(inner.astype(BF16), vh) + _dot(qh, spad) * jnp.exp((row + 1.0) * lg)
        o_ref[0, :, DV_R * h:DV_R * (h + 1)] = o.astype(BF16)
        kd = (kpair.astype(F32) * jnp.exp((C - 1.0 - row) * lg)).astype(BF16)
        upd = _dot_tn(kd, vh)
        s_ref[h] = sh * math.exp(C * lg) + upd[DK_R * s:DK_R * (s + 1), :]
    sfin_ref[0] = s_ref[...]


def _retention(qr, kr, vr, s0, C):
    B, T, _ = qr.shape
    blk = lambda n: pl.BlockSpec((1, C, n), lambda b, c: (b, c, 0))
    st = pl.BlockSpec((1, N_HEADS_R, DK_R, DV_R), lambda b, c: (b, 0, 0, 0))
    return pl.pallas_call(
        functools.partial(_ret_kernel, C=C),
        grid=(B, T // C),
        in_specs=[blk(WQR), blk(WQR), blk(WVR), st],
        out_specs=[blk(WVR), st],
        out_shape=[jax.ShapeDtypeStruct((B, T, WVR), BF16),
                   jax.ShapeDtypeStruct((B, N_HEADS_R, DK_R, DV_R), F32)],
        scratch_shapes=[pltpu.VMEM((N_HEADS_R, DK_R, DV_R), F32)],
        compiler_params=_params(("parallel", "arbitrary")),
        name="retention",
    )(qr, kr, vr, s0)


def _merge_kernel(x_ref, mod_ref, oa_ref, orr_ref, gr_ref, ga_ref, gb_ref, rg_ref, wpa_ref, wpb_ref,
                  wout_ref, n2g_ref, wrh_ref, wrl_ref, br_ref, x1_ref, h2_ref, comb_ref):
    orr = orr_ref[0].astype(F32)
    parts = []
    for h in range(N_HEADS_R):
        oh = orr[:, DV_R * h:DV_R * (h + 1)]
        parts.append(oh * lax.rsqrt(jnp.mean(oh * oh, axis=-1, keepdims=True) + EPS))
    gr = gr_ref[0].astype(F32)
    on = jnp.concatenate(parts, axis=1) * rg_ref[...] * (gr * _sigmoid(gr))
    a = _dot(oa_ref[0], wpa_ref[...])
    b = _dot(on.astype(BF16), wpb_ref[...])
    merged = _sigmoid(ga_ref[0].astype(F32)) * a + _sigmoid(gb_ref[0].astype(F32)) * b
    y = _dot(merged.astype(BF16), wout_ref[...])
    x1 = x_ref[0] + mod_ref[0, 2:3, :] * y
    x1_ref[0] = x1
    h2 = x1 * lax.rsqrt(jnp.mean(x1 * x1, axis=-1, keepdims=True) + EPS) * n2g_ref[...]
    h2 = h2 * (1.0 + mod_ref[0, 4:5, :]) + mod_ref[0, 3:4, :]
    h2_ref[0] = h2.astype(BF16)

    h_hi = h2.astype(BF16)
    h_lo = (h2 - h_hi.astype(F32)).astype(BF16)
    lgt = _dot(h_hi, wrh_ref[...]) + _dot(h_hi, wrl_ref[...]) + _dot(h_lo, wrh_ref[...]) + br_ref[...]
    lane = lax.broadcasted_iota(jnp.int32, lgt.shape, 1)
    big = jnp.int32(1 << 20)

    def rmax(x):
        return jnp.max(x, axis=-1, keepdims=True)

    def first_lane(m):
        return jnp.min(jnp.where(m, lane, big), axis=-1, keepdims=True)

    gmask = lane < N_GROUPS
    gl = jnp.where(gmask, lgt, -jnp.inf)
    gmax = rmax(gl)
    g_top_p = 1.0 / jnp.sum(jnp.exp(gl - gmax), axis=-1, keepdims=True)
    g_idx = first_lane(jnp.logical_and(gmask, gl == gmax))
    el = lane - N_GROUPS
    emask = jnp.logical_and(jnp.logical_and(el >= 0, el < N_EXPERTS), (el // EXPERTS_PER_GROUP) == g_idx)
    e1 = jnp.where(emask, lgt, -jnp.inf)
    l1 = rmax(e1)
    i1 = first_lane(jnp.logical_and(emask, e1 == l1))
    e2 = jnp.where(lane == i1, -jnp.inf, e1)
    l2 = rmax(e2)
    i2 = first_lane(jnp.logical_and(emask, jnp.logical_and(e2 == l2, lane != i1)))
    r = jnp.exp(l2 - l1)
    w1 = g_top_p / (1.0 + r)
    comb_ref[0] = jnp.where(lane == i1, w1, jnp.where(lane == i2, w1 * r, 0.0))


def _merge(x, mod, oa, orr, gr, ga, gb, rg, wpa, wpb, wout, n2g, wrh, wrl, br, tm):
    B, T, _ = x.shape
    tok = lambda n: pl.BlockSpec((1, tm, n), lambda b, i: (b, i, 0))
    const = lambda r, c: pl.BlockSpec((r, c), lambda b, i: (0, 0))
    return pl.pallas_call(
        _merge_kernel,
        grid=(B, T // tm),
        in_specs=[tok(D_MODEL), pl.BlockSpec((1, 6, D_MODEL), lambda b, i: (b, 0, 0)),
                  tok(WA), tok(WVR), tok(WVR), tok(D_MODEL), tok(D_MODEL),
                  const(1, WVR), const(WA, D_MODEL), const(WVR, D_MODEL), const(D_MODEL, D_MODEL),
                  const(1, D_MODEL), const(D_MODEL, LANES), const(D_MODEL, LANES), const(1, LANES)],
        out_specs=[tok(D_MODEL), tok(D_MODEL), tok(LANES)],
        out_shape=[jax.ShapeDtypeStruct((B, T, D_MODEL), F32),
                   jax.ShapeDtypeStruct((B, T, D_MODEL), BF16),
                   jax.ShapeDtypeStruct((B, T, LANES), F32)],
        compiler_params=_params(("parallel", "arbitrary")),
        name="merge",
    )(x, mod, oa, orr, gr, ga, gb, rg, wpa, wpb, wout, n2g, wrh, wrl, br)


def _moe_kernel(h2_ref, comb_ref, x1_ref, mod_ref, wg_ref, wu_ref, wd_ref, y_ref, acc_ref):
    e = pl.program_id(2)

    @pl.when(e == 0)
    def _():
        acc_ref[...] = jnp.zeros_like(acc_ref)

    hb = h2_ref[0]
    g = _dot(hb, wg_ref[0])
    he = (g * _sigmoid(g)) * _dot(hb, wu_ref[0])
    comb = comb_ref[0]
    lane = lax.broadcasted_iota(jnp.int32, comb.shape, 1)
    ce = jnp.sum(jnp.where(lane == e + N_GROUPS, comb, 0.0), axis=-1, keepdims=True)
    acc_ref[...] += ce * _dot(he.astype(BF16), wd_ref[0])

    @pl.when(e == N_EXPERTS - 1)
    def _():
        y_ref[0] = x1_ref[0] + mod_ref[0, 5:6, :] * acc_ref[...]


def _moe(h2, comb, x1, mod, wg, wu, wd, tm):
    B, T, _ = h2.shape
    tok = lambda n: pl.BlockSpec((1, tm, n), lambda b, i, e: (b, i, 0))
    return pl.pallas_call(
        _moe_kernel,
        grid=(B, T // tm, N_EXPERTS),
        in_specs=[tok(D_MODEL), tok(LANES), tok(D_MODEL),
                  pl.BlockSpec((1, 6, D_MODEL), lambda b, i, e: (b, 0, 0)),
                  pl.BlockSpec((1, D_MODEL, D_FF_E), lambda b, i, e: (e, 0, 0)),
                  pl.BlockSpec((1, D_MODEL, D_FF_E), lambda b, i, e: (e, 0, 0)),
                  pl.BlockSpec((1, D_FF_E, D_MODEL), lambda b, i, e: (e, 0, 0))],
        out_specs=tok(D_MODEL),
        out_shape=jax.ShapeDtypeStruct((B, T, D_MODEL), F32),
        scratch_shapes=[pltpu.VMEM((tm, D_MODEL), F32)],
        compiler_params=_params(("parallel", "parallel", "arbitrary")),
        name="moe",
    )(h2, comb, x1, mod, wg, wu, wd)


def _pad_rows(a, n):
    return a if a.shape[1] == n else jnp.pad(a, ((0, 0), (0, n - a.shape[1]), (0, 0)))


def _group_layer(x, mod, past_k, past_v, past_ki, s0, w, *, tm, QB, TK, C):
    B, T, _ = x.shape
    P = past_k.shape[1]
    L = P + T
    no_past = P == 0
    assert not no_past or tm == TK, "the in-projection's token tile is the attention key tile"
    (q, kf, kb, vf, vb, qi, kif, kib, wi, qr, kr, vr, gr, ga, gb) = _inproj(
        x, mod, w["n1g"], w["w_in"], w["bd"], w["qg"], w["kg"], w["kig"], tm, keys_transposed=no_past)
    Lp = -(-L // TK) * TK
    if no_past:
        kt_all, v_all, kit_all = kb, vb, kib
    else:
        def tiles(a):
            a = _pad_rows(a, Lp)
            return a.reshape(B, Lp // TK, TK, a.shape[-1]).transpose(0, 1, 3, 2)
        pki = past_ki.astype(BF16)
        kt_all = tiles(jnp.concatenate([past_k.reshape(B, P, WA).astype(BF16), kb], axis=1))
        kit_all = tiles(jnp.concatenate([jnp.concatenate([pki, pki], axis=-1), kib], axis=1))
        v_all = _pad_rows(jnp.concatenate([past_v.reshape(B, P, WA).astype(BF16), vb], axis=1), Lp)
    oa = _dsa(q, qi, wi, kt_all, v_all, kit_all, w["tri"], L=L, pos0=P, QB=QB)
    orr, s_new = _retention(qr, kr, vr, s0, C)
    x1, h2, comb = _merge(x, mod, oa, orr, gr, ga, gb, w["rg"], w["wpa"], w["wpb"], w["wout"],
                          w["n2g"], w["wrh"], w["wrl"], w["br"], tm)
    y = _moe(h2, comb, x1, mod, w["wg"], w["wu"], w["wd"], tm)
    return y, (kf.reshape(B, T, N_HEADS_A, HEAD_DIM_A), vf.reshape(B, T, N_HEADS_A, HEAD_DIM_A), kif, s_new)


def _pack_w_in(w_in):
    o = 0
    parts = []
    for n in (WA, WA, WA, WQI):
        parts.append(w_in[:, o:o + n])
        o += n
    parts.append(w_in[:, o:o + IDX_DIM + IDX_HEADS])
    o += IDX_DIM + IDX_HEADS
    parts.append(jnp.zeros((D_MODEL, LANES - IDX_DIM - IDX_HEADS), w_in.dtype))
    parts.append(w_in[:, o:])
    return jnp.concatenate(parts, axis=1).astype(BF16)


def kernel(x_prompt, x_sample, cache_k, cache_v, cache_kidx, state_ret, c_prompt, c_sample, w_ada, b_ada,
           norm1_g, norm2_g, w_in, qn_g, kn_g, kidx_g, ret_norm_g, w_pa, w_pb, w_out, w_rg, b_rg, w_re, b_re,
           w_e_gate, w_e_up, w_e_down):
    depth = w_ada.shape[0]
    bp, bs = x_prompt.shape[0], x_sample.shape[0]
    TK = 512
    gid = jnp.arange(WA, dtype=jnp.int32) // HEAD_DIM_A
    bd = (gid[:, None] == gid[None, :]).astype(BF16)
    kk = jnp.arange(TK, dtype=jnp.int32)
    tri = (kk[:, None] < kk[None, :]).astype(BF16)

    yp, ys = x_prompt, x_sample
    outs = [[] for _ in range(8)]
    for l in range(depth):
        wr = jnp.concatenate([w_rg[l], w_re[l],
                              jnp.zeros((D_MODEL, LANES - N_GROUPS - N_EXPERTS), F32)], axis=1)
        wrh = wr.astype(BF16)
        w = dict(
            n1g=norm1_g[l].reshape(1, D_MODEL), n2g=norm2_g[l].reshape(1, D_MODEL),
            w_in=_pack_w_in(w_in[l]), bd=bd, tri=tri,
            qg=jnp.tile(qn_g[l], N_HEADS_A).reshape(1, WA), kg=jnp.tile(kn_g[l], N_HEADS_A).reshape(1, WA),
            kig=jnp.concatenate([kidx_g[l], jnp.zeros((LANES - IDX_DIM,), F32)]).reshape(1, LANES),
            rg=jnp.tile(ret_norm_g[l], N_HEADS_R).reshape(1, WVR),
            wpa=w_pa[l].astype(BF16), wpb=w_pb[l].astype(BF16), wout=w_out[l].astype(BF16),
            wrh=wrh, wrl=(wr - wrh.astype(F32)).astype(BF16),
            br=jnp.concatenate([b_rg[l], b_re[l], jnp.zeros((LANES - N_GROUPS - N_EXPERTS,), F32)]).reshape(1, LANES),
            wg=w_e_gate[l].astype(BF16), wu=w_e_up[l].astype(BF16), wd=w_e_down[l].astype(BF16),
        )
        rows = -(-(bp + bs) // 8) * 8
        c_all = jnp.concatenate([c_prompt, c_sample, jnp.zeros((rows - bp - bs, D_MODEL), F32)], axis=0)
        mod = _ada(c_all, w_ada[l], b_ada[l]).reshape(rows, 6, D_MODEL)
        tp, ts = yp.shape[1], ys.shape[1]
        empty_kv = jnp.zeros((bp, 0, N_HEADS_A, HEAD_DIM_A), F32)
        empty_ki = jnp.zeros((bp, 0, IDX_DIM), F32)
        s_zero = jnp.zeros((bp, N_HEADS_R, DK_R, DV_R), F32)
        yp, st_p = _group_layer(yp, mod[:bp], empty_kv, empty_kv, empty_ki, s_zero, w,
                                tm=min(512, tp), QB=min(512, tp), TK=TK, C=min(256, tp))
        ys, st_s = _group_layer(ys, mod[bp:bp + bs], cache_k[l], cache_v[l], cache_kidx[l], state_ret[l], w,
                                tm=min(512, ts), QB=min(128, ts), TK=TK, C=min(256, ts))
        for i, a in enumerate(st_p + st_s):
            outs[i].append(a)
    return (yp, ys) + tuple(jnp.stack(o) for o in outs)
```
